```python
import math
import jax, jax.numpy as jnp
from jax import lax
import numpy as np

D_MODEL = 2048
BATCH = 4
SEQ = 2048
DEPTH = 1

CHUNK = 64
D_S5 = D_MODEL // 2
S5_GROUP = 16
S5_GROUPS = D_S5 // S5_GROUP
S5_STATE = 64
DN_HEADS = 8
DN_HEAD_DIM = 128
D_DN = DN_HEADS * DN_HEAD_DIM
CONV_K = 4
EPS = 1e-6
IN_SPLIT_SIZES = (D_S5, D_S5, D_DN, D_DN, D_DN, D_DN, DN_HEADS, DN_HEADS, D_MODEL, D_MODEL)
D_IN = 2 * D_S5 + 4 * D_DN + 2 * DN_HEADS + 2 * D_MODEL

kernel_name = "hybrid_s5_gated_deltanet_block"


def _f32(t):
    return t.astype(jnp.float32)


def rmsnorm(x, w):
    xf = _f32(x)
    return xf * lax.rsqrt(jnp.mean(xf * xf, axis=-1, keepdims=True) + EPS) * _f32(w)


def l2norm(t):
    return t * lax.rsqrt(jnp.sum(t * t, axis=-1, keepdims=True) + EPS)


def causal_depthwise_conv(x, w):
    c = x.shape[-1]
    return lax.conv_general_dilated(
        x, w[:, None, :], window_strides=(1,), padding=[(CONV_K - 1, 0)],
        dimension_numbers=("NWC", "WIO", "NWC"), feature_group_count=c)


def s5_mixer(u, z, lam_re, lam_im, log_step, b_re, b_im, c_re, c_im, d_skip, w_glu):
    bsz, l, _ = u.shape
    lam_re, lam_im = _f32(lam_re), _f32(lam_im)
    step = jnp.exp(_f32(log_step))[:, None]
    mag = jnp.exp(lam_re * step)
    abar_re = mag * jnp.cos(lam_im * step)
    abar_im = mag * jnp.sin(lam_im * step)
    den = lam_re * lam_re + lam_im * lam_im
    xr = abar_re - 1.0
    f_re = (xr * lam_re + abar_im * lam_im) / den
    f_im = (abar_im * lam_re - xr * lam_im) / den
    b_re, b_im = _f32(b_re), _f32(b_im)
    bb_re = f_re[..., None] * b_re - f_im[..., None] * b_im
    bb_im = f_re[..., None] * b_im + f_im[..., None] * b_re
    ug = u.reshape(bsz, l, S5_GROUPS, S5_GROUP)
    bu_re = jnp.einsum("blgc,gpc->blgp", ug, bb_re)
    bu_im = jnp.einsum("blgc,gpc->blgp", ug, bb_im)
    a_re = jnp.broadcast_to(abar_re, bu_re.shape)
    a_im = jnp.broadcast_to(abar_im, bu_im.shape)

    def combine(e1, e2):
        a1r, a1i, b1r, b1i = e1
        a2r, a2i, b2r, b2i = e2
        return (a2r * a1r - a2i * a1i,
                a2r * a1i + a2i * a1r,
                a2r * b1r - a2i * b1i + b2r,
                a2r * b1i + a2i * b1r + b2i)

    _, _, s_re, s_im = lax.associative_scan(combine, (a_re, a_im, bu_re, bu_im), axis=1)
    y = (jnp.einsum("blgp,gcp->blgc", s_re, _f32(c_re))
         - jnp.einsum("blgp,gcp->blgc", s_im, _f32(c_im)))
    y = y.reshape(bsz, l, D_S5) + _f32(d_skip) * u
    y = jax.nn.gelu(y)
    y = y * jax.nn.sigmoid(y @ _f32(w_glu))
    return y * jax.nn.silu(z)


def gated_delta_rule(q, k, v, g, beta):
    bsz, l, h, dk = q.shape
    dv = v.shape[-1]
    n = l // CHUNK

    def chunks(t):
        return t.reshape(bsz, n, CHUNK, h, -1).transpose(0, 3, 1, 2, 4)

    q = chunks(q) * (dk ** -0.5)
    k = chunks(k)
    v = chunks(v)
    g = g.reshape(bsz, n, CHUNK, h).transpose(0, 3, 1, 2)
    beta = beta.reshape(bsz, n, CHUNK, h).transpose(0, 3, 1, 2)
    gc = jnp.cumsum(g, axis=-1)
    causal = jnp.tril(jnp.ones((CHUNK, CHUNK), dtype=bool))
    strict = jnp.tril(jnp.ones((CHUNK, CHUNK), dtype=bool), -1)
    decay = jnp.exp(jnp.where(causal, gc[..., :, None] - gc[..., None, :], -jnp.inf))
    kk = jnp.einsum("bhncd,bhnsd->bhncs", k, k)
    a_mat = jnp.where(strict, beta[..., None] * kk * decay, 0.0)
    rhs = jnp.concatenate([v * beta[..., None], k * (beta * jnp.exp(gc))[..., None]], axis=-1)
    sol = lax.linalg.triangular_solve(a_mat, rhs, left_side=True, lower=True,
                                      unit_diagonal=True)
    u_c, w_c = sol[..., :dv], sol[..., dv:]
    qk = jnp.einsum("bhncd,bhnsd->bhncs", q, k) * decay
    q_dec = q * jnp.exp(gc)[..., None]
    k_dec = k * jnp.exp(gc[..., -1:] - gc)[..., None]
    g_last = jnp.exp(gc[..., -1])

    def step(state, inp):
        u_i, w_i, qk_i, qd_i, kd_i, gl_i = inp
        v_new = u_i - jnp.einsum("bhcd,bhde->bhce", w_i, state)
        o = (jnp.einsum("bhcd,bhde->bhce", qd_i, state)
             + jnp.einsum("bhcs,bhse->bhce", qk_i, v_new))
        state = state * gl_i[..., None, None] + jnp.einsum("bhcd,bhce->bhde", kd_i, v_new)
        return state, o

    xs = (jnp.moveaxis(u_c, 2, 0), jnp.moveaxis(w_c, 2, 0), jnp.moveaxis(qk, 2, 0),
          jnp.moveaxis(q_dec, 2, 0), jnp.moveaxis(k_dec, 2, 0), jnp.moveaxis(g_last, 2, 0))
    s0 = jnp.zeros((bsz, h, dk, dv), q.dtype)
    _, o = lax.scan(step, s0, xs)
    return o.transpose(1, 0, 3, 2, 4).reshape(bsz, l, h, dv)


def deltanet_mixer(q, k, v, z, beta_logit, a_logit, conv_w, a_log, dt_bias, norm_w):
    bsz, l, _ = q.shape
    qkv = jax.nn.silu(causal_depthwise_conv(jnp.concatenate([q, k, v], axis=-1), _f32(conv_w)))
    q, k, v = jnp.split(qkv, [D_DN, 2 * D_DN], axis=-1)
    q = l2norm(q.reshape(bsz, l, DN_HEADS, DN_HEAD_DIM))
    k = l2norm(k.reshape(bsz, l, DN_HEADS, DN_HEAD_DIM))
    v = v.reshape(bsz, l, DN_HEADS, DN_HEAD_DIM)
    beta = jax.nn.sigmoid(beta_logit)
    g = -jnp.exp(_f32(a_log)) * jax.nn.softplus(a_logit + _f32(dt_bias))
    o = gated_delta_rule(q, k, v, g, beta)
    o = rmsnorm(o, norm_w) * jax.nn.silu(z.reshape(bsz, l, DN_HEADS, DN_HEAD_DIM))
    return o.reshape(bsz, l, D_DN)


def setup_inputs(seed: int = 0) -> dict:
    key = jax.random.key(seed)
    ks = jax.random.split(key, 24)
    f = jnp.float32
    x = jax.random.normal(ks[0], (BATCH, SEQ, D_MODEL), f)
    ln_w = 1.0 + 0.01 * jax.random.normal(ks[1], (DEPTH, D_MODEL), f)
    w_in = jax.random.normal(ks[2], (DEPTH, D_MODEL, D_IN), f) * D_MODEL ** -0.5
    n_idx = jnp.arange(S5_STATE, dtype=f)
    s5_lam_re = -0.5 + 0.01 * jax.random.normal(ks[3], (DEPTH, S5_GROUPS, S5_STATE), f)
    s5_lam_im = math.pi * n_idx + 0.01 * jax.random.normal(ks[4], (DEPTH, S5_GROUPS, S5_STATE), f)
    s5_log_step = jax.random.uniform(ks[5], (DEPTH, S5_GROUPS), f, math.log(1e-3), math.log(1e-1))
    bsc = (2.0 * S5_GROUP) ** -0.5
    s5_b_re = jax.random.normal(ks[6], (DEPTH, S5_GROUPS, S5_STATE, S5_GROUP), f) * bsc
    s5_b_im = jax.random.normal(ks[7], (DEPTH, S5_GROUPS, S5_STATE, S5_GROUP), f) * bsc
    csc = (2.0 * S5_STATE) ** -0.5
    s5_c_re = jax.random.normal(ks[8], (DEPTH, S5_GROUPS, S5_GROUP, S5_STATE), f) * csc
    s5_c_im = jax.random.normal(ks[9], (DEPTH, S5_GROUPS, S5_GROUP, S5_STATE), f) * csc
    s5_d = jax.random.normal(ks[10], (DEPTH, D_S5), f)
    s5_w_glu = jax.random.normal(ks[11], (DEPTH, D_S5, D_S5), f) * D_S5 ** -0.5
    s5_w_up = jax.random.normal(ks[12], (DEPTH, D_S5, D_MODEL), f) * D_S5 ** -0.5
    dn_conv_w = jax.random.normal(ks[13], (DEPTH, CONV_K, 3 * D_DN), f) * CONV_K ** -0.5
    dn_a_log = jnp.log(jax.random.uniform(ks[14], (DEPTH, DN_HEADS), f, 1.0, 16.0))
    dt = jnp.exp(jax.random.uniform(ks[15], (DEPTH, DN_HEADS), f, math.log(1e-3), math.log(1e-1)))
    dn_dt_bias = dt + jnp.log(-jnp.expm1(-dt))
    dn_norm_w = 1.0 + 0.01 * jax.random.normal(ks[16], (DEPTH, DN_HEAD_DIM), f)
    dn_w_up = jax.random.normal(ks[17], (DEPTH, D_DN, D_MODEL), f) * D_DN ** -0.5
    w_out = jax.random.normal(ks[18], (DEPTH, D_MODEL, D_MODEL), f) * D_MODEL ** -0.5
    final_norm_w = 1.0 + 0.01 * jax.random.normal(ks[19], (D_MODEL,), f)
    return {"x": x, "ln_w": ln_w, "w_in": w_in, "s5_lam_re": s5_lam_re, "s5_lam_im": s5_lam_im,
            "s5_log_step": s5_log_step, "s5_b_re": s5_b_re, "s5_b_im": s5_b_im,
            "s5_c_re": s5_c_re, "s5_c_im": s5_c_im, "s5_d": s5_d, "s5_w_glu": s5_w_glu,
            "s5_w_up": s5_w_up, "dn_conv_w": dn_conv_w, "dn_a_log": dn_a_log,
            "dn_dt_bias": dn_dt_bias, "dn_norm_w": dn_norm_w, "dn_w_up": dn_w_up,
            "w_out": w_out, "final_norm_w": final_norm_w}


def reference(x, ln_w, w_in, s5_lam_re, s5_lam_im, s5_log_step, s5_b_re, s5_b_im,
              s5_c_re, s5_c_im, s5_d, s5_w_glu, s5_w_up, dn_conv_w, dn_a_log,
              dn_dt_bias, dn_norm_w, dn_w_up, w_out, final_norm_w):
    split_points = np.cumsum(np.array(IN_SPLIT_SIZES))[:-1].tolist()
    for layer in range(DEPTH):
        h = rmsnorm(x, ln_w[layer])
        proj = _f32(h @ _f32(w_in[layer]))
        (u_s, z_s, q, k, v, z_d, beta_l, a_l, gate_s, gate_d) = jnp.split(proj, split_points, axis=-1)
        y_s = s5_mixer(u_s, z_s, s5_lam_re[layer], s5_lam_im[layer], s5_log_step[layer],
                       s5_b_re[layer], s5_b_im[layer], s5_c_re[layer], s5_c_im[layer],
                       s5_d[layer], s5_w_glu[layer]) @ _f32(s5_w_up[layer])
        y_d = deltanet_mixer(q, k, v, z_d, beta_l, a_l, dn_conv_w[layer], dn_a_log[layer],
                             dn_dt_bias[layer], dn_norm_w[layer]) @ _f32(dn_w_up[layer])
        mixed = jax.nn.sigmoid(gate_s) * y_s + jax.nn.sigmoid(gate_d) * y_d
        x = x + (mixed @ _f32(w_out[layer])).astype(x.dtype)
    return rmsnorm(x, final_norm_w).astype(x.dtype)
```

```python
import functools
import math

import jax
import jax.numpy as jnp
from jax import lax
from jax.experimental import pallas as pl
from jax.experimental.pallas import tpu as pltpu

D_MODEL = 2048
CHUNK = 64
D_S5 = 1024
S5_GROUP = 16
S5_GROUPS = 64
S5_STATE = 64
N_STATE = S5_GROUPS * S5_STATE
DN_HEADS = 8
DN_HEAD_DIM = 128
D_DN = 1024
CONV_K = 4
EPS = 1e-6
N_SMALL = 2 * DN_HEADS
D_MAIN = 2 * D_S5 + 4 * D_DN + 2 * D_MODEL
SMALL_PAD = 128

SUBLANES = 8
LANES = 128
VMEM_LIMIT = 56 * 1024 * 1024

BF16 = jnp.bfloat16
F32 = jnp.float32
HIGHEST = lax.Precision.HIGHEST


def _dot(a, b):
    return jnp.dot(a.astype(BF16), b.astype(BF16), preferred_element_type=F32)


def _dot_nt(a, b):
    return lax.dot_general(a.astype(BF16), b.astype(BF16), (((1,), (1,)), ((), ())),
                           preferred_element_type=F32)


def _dot_tn(a, b):
    return lax.dot_general(a.astype(BF16), b.astype(BF16), (((0,), (0,)), ((), ())),
                           preferred_element_type=F32)


def _dot_f32(a, b):
    return jnp.dot(a, b, preferred_element_type=F32, precision=HIGHEST)


def _sigmoid(x):
    return 1.0 / (1.0 + jnp.exp(-x))


def _silu(x):
    return x * _sigmoid(x)


def _in_proj_kernel(x_ref, lnw_ref, wm_ref, ws_ref, om_ref, os_ref, h_ref):
    j = pl.program_id(1)

    @pl.when(j == 0)
    def _():
        x = x_ref[...]
        ms = jnp.mean(x * x, axis=-1, keepdims=True)
        h = (x * lax.rsqrt(ms + EPS) * lnw_ref[...]).astype(BF16)
        h_ref[...] = h
        os_ref[...] = jnp.dot(h, ws_ref[...], preferred_element_type=F32)

    om_ref[...] = jnp.dot(h_ref[...], wm_ref[...], preferred_element_type=F32)


def _in_proj(x2, ln_w, w_main, w_small, tm=1024, tn=512):
    t = x2.shape[0]
    grid = (t // tm, D_MAIN // tn)
    return pl.pallas_call(
        _in_proj_kernel,
        grid=grid,
        in_specs=[
            pl.BlockSpec((tm, D_MODEL), lambda i, j: (i, 0)),
            pl.BlockSpec((1, D_MODEL), lambda i, j: (0, 0)),
            pl.BlockSpec((D_MODEL, tn), lambda i, j: (0, j)),
            pl.BlockSpec((D_MODEL, SMALL_PAD), lambda i, j: (0, 0)),
        ],
        out_specs=[
            pl.BlockSpec((tm, tn), lambda i, j: (i, j)),
            pl.BlockSpec((tm, SMALL_PAD), lambda i, j: (i, 0)),
        ],
        out_shape=[
            jax.ShapeDtypeStruct((t, D_MAIN), F32),
            jax.ShapeDtypeStruct((t, SMALL_PAD), F32),
        ],
        scratch_shapes=[pltpu.VMEM((tm, D_MODEL), BF16)],
        compiler_params=pltpu.CompilerParams(
            dimension_semantics=("arbitrary", "arbitrary"), vmem_limit_bytes=VMEM_LIMIT),
        name="in_proj",
    )(x2, ln_w, w_main, w_small)


def _s5_disc_kernel(lre_ref, lim_ref, step_ref, bre_ref, bim_ref,
                    are_ref, aim_ref, bbre_ref, bbim_ref):
    lre = lre_ref[...]
    lim = lim_ref[...]
    step = jnp.exp(step_ref[...])
    mag = jnp.exp(lre * step)
    are = mag * jnp.cos(lim * step)
    aim = mag * jnp.sin(lim * step)
    den = lre * lre + lim * lim
    xr = are - 1.0
    fre = (xr * lre + aim * lim) / den
    fim = (aim * lre - xr * lim) / den
    bre = bre_ref[...]
    bim = bim_ref[...]
    are_ref[...] = are
    aim_ref[...] = aim
    bbre_ref[...] = fre * bre - fim * bim
    bbim_ref[...] = fre * bim + fim * bre


def _s5_discretise(lam_re, lam_im, log_step, b_re, b_im):
    g, p = lam_re.shape
    c = b_re.shape[-1]
    step = jnp.broadcast_to(log_step[:, None, None], (g, 1, p))
    b_re_t = jnp.swapaxes(b_re, 1, 2)
    b_im_t = jnp.swapaxes(b_im, 1, 2)
    out_shape = [jax.ShapeDtypeStruct((g, 1, p), F32)] * 2 + [jax.ShapeDtypeStruct((g, c, p), F32)] * 2
    return pl.pallas_call(_s5_disc_kernel, out_shape=out_shape, name="s5_disc")(
        lam_re[:, None, :], lam_im[:, None, :], step, b_re_t, b_im_t)


def _cmul(ar, ai, br, bi):
    return ar * br - ai * bi, ar * bi + ai * br


def _s5_kernel(u_ref, z_ref, bbre_ref, bbim_ref, ccre_ref, ccim_ref, coef_ref, d_ref, wglu_ref,
               o_ref, sre_ref, sim_ref, *, tc):
    it = pl.program_id(1)
    nb = tc // SUBLANES

    @pl.when(it == 0)
    def _():
        sre_ref[0:SUBLANES, :] = jnp.zeros((SUBLANES, N_STATE), F32)
        sim_ref[0:SUBLANES, :] = jnp.zeros((SUBLANES, N_STATE), F32)

    u = u_ref[...]
    ub = u.astype(BF16)
    gpc = LANES // S5_GROUP
    wst = gpc * S5_STATE
    for k in range(D_S5 // LANES):
        uk = ub[:, k * LANES:(k + 1) * LANES]
        sre_ref[SUBLANES:, k * wst:(k + 1) * wst] = jnp.dot(uk, bbre_ref[k], preferred_element_type=F32)
        sim_ref[SUBLANES:, k * wst:(k + 1) * wst] = jnp.dot(uk, bbim_ref[k], preferred_element_type=F32)

    lw = 512

    def scan_block(rb, carry):
        r0 = pl.multiple_of(rb * SUBLANES, SUBLANES)
        for c in range(N_STATE // lw):
            ls = slice(c * lw, (c + 1) * lw)
            xr = sre_ref[pl.ds(r0, SUBLANES), ls]
            xi = sim_ref[pl.ds(r0, SUBLANES), ls]
            for si, sh in enumerate((1, 2, 4)):
                cr = coef_ref[2 * si, :, ls]
                ci = coef_ref[2 * si + 1, :, ls]
                tr = pltpu.roll(xr, sh, 0)
                ti = pltpu.roll(xi, sh, 0)
                dr, di = _cmul(cr, ci, tr, ti)
                xr = xr + dr
                xi = xi + di
            pr = jnp.broadcast_to(sre_ref[pl.ds(r0 - 1, 1), ls], (SUBLANES, lw))
            pi = jnp.broadcast_to(sim_ref[pl.ds(r0 - 1, 1), ls], (SUBLANES, lw))
            dr, di = _cmul(coef_ref[6, :, ls], coef_ref[7, :, ls], pr, pi)
            sre_ref[pl.ds(r0, SUBLANES), ls] = xr + dr
            sim_ref[pl.ds(r0, SUBLANES), ls] = xi + di
        return carry

    lax.fori_loop(1, nb + 1, scan_block, 0)

    gpc2 = 16
    wst2 = gpc2 * S5_STATE
    ys = []
    for k in range(S5_GROUPS // gpc2):
        sr = sre_ref[SUBLANES:, k * wst2:(k + 1) * wst2].astype(BF16)
        si_ = sim_ref[SUBLANES:, k * wst2:(k + 1) * wst2].astype(BF16)
        ys.append(jnp.dot(sr, ccre_ref[k], preferred_element_type=F32)
                  - jnp.dot(si_, ccim_ref[k], preferred_element_type=F32))
    y = jnp.concatenate(ys, axis=-1) + d_ref[...] * u
    y = jax.nn.gelu(y)
    y = y * _sigmoid(_dot(y, wglu_ref[...]))
    o_ref[...] = (y * _silu(z_ref[...])).astype(o_ref.dtype)

    sre_ref[0:SUBLANES, :] = sre_ref[tc:tc + SUBLANES, :]
    sim_ref[0:SUBLANES, :] = sim_ref[tc:tc + SUBLANES, :]


def _s5_branch(proj, batch, seq, bb_re, bb_im, cc_re, cc_im, coef, d_skip, w_glu, tc=256):
    nt = seq // tc
    kern = functools.partial(_s5_kernel, tc=tc)
    const3 = lambda b, i: (0, 0, 0)
    return pl.pallas_call(
        kern,
        grid=(batch, nt),
        in_specs=[
            pl.BlockSpec((tc, D_S5), lambda b, i: (b * nt + i, 0)),
            pl.BlockSpec((tc, D_S5), lambda b, i: (b * nt + i, 1)),
            pl.BlockSpec(bb_re.shape, const3),
            pl.BlockSpec(bb_im.shape, const3),
            pl.BlockSpec(cc_re.shape, const3),
            pl.BlockSpec(cc_im.shape, const3),
            pl.BlockSpec(coef.shape, const3),
            pl.BlockSpec((1, D_S5), lambda b, i: (0, 0)),
            pl.BlockSpec((D_S5, D_S5), lambda b, i: (0, 0)),
        ],
        out_specs=pl.BlockSpec((tc, D_S5), lambda b, i: (b * nt + i, 0)),
        out_shape=jax.ShapeDtypeStruct((batch * seq, D_S5), BF16),
        scratch_shapes=[pltpu.VMEM((tc + SUBLANES, N_STATE), F32),
                        pltpu.VMEM((tc + SUBLANES, N_STATE), F32)],
        compiler_params=pltpu.CompilerParams(
            dimension_semantics=("arbitrary", "arbitrary"), vmem_limit_bytes=VMEM_LIMIT),
        name="s5_branch",
    )(proj, proj, bb_re, bb_im, cc_re, cc_im, coef, d_skip, w_glu)


def _s5_weights(lam_re, lam_im, log_step, b_re, b_im, c_re, c_im):
    are, aim, bbre_t, bbim_t = _s5_discretise(lam_re, lam_im, log_step, b_re, b_im)
    g, c, p = bbre_t.shape

    def blockdiag_in(m):
        gb = LANES // c
        m = m.reshape(g // gb, gb, c, p)
        eye = jnp.eye(gb, dtype=m.dtype)
        return jnp.einsum("kgcp,gh->kgchp", m, eye).reshape(g // gb, gb * c, gb * p).astype(BF16)

    def blockdiag_out(m):
        gb = 16
        m = jnp.swapaxes(m, 1, 2).reshape(g // gb, gb, p, c)
        eye = jnp.eye(gb, dtype=m.dtype)
        return jnp.einsum("kgpc,gh->kgphc", m, eye).reshape(g // gb, gb * p, gb * c).astype(BF16)

    ar = are.reshape(1, g * p)
    ai = aim.reshape(1, g * p)
    a2r, a2i = _cmul(ar, ai, ar, ai)
    a4r, a4i = _cmul(a2r, a2i, a2r, a2i)
    rows = jnp.arange(SUBLANES)[:, None]
    coefs = []
    for sh, (cr, ci) in ((1, (ar, ai)), (2, (a2r, a2i)), (4, (a4r, a4i))):
        mask = rows >= sh
        coefs += [jnp.where(mask, cr, 0.0), jnp.where(mask, ci, 0.0)]
    pr, pi = [ar], [ai]
    for _ in range(SUBLANES - 1):
        nr, ni = _cmul(pr[-1], pi[-1], ar, ai)
        pr.append(nr)
        pi.append(ni)
    coefs += [jnp.concatenate(pr, axis=0), jnp.concatenate(pi, axis=0)]
    coef = jnp.stack(coefs, axis=0)
    return (blockdiag_in(bbre_t), blockdiag_in(bbim_t),
            blockdiag_out(c_re), blockdiag_out(c_im), coef)


def _dn_kernel(q_ref, k_ref, v_ref, z_ref, ps_ref, cw_ref, gpar_ref, expb_ref, expg_ref, nw_ref,
               o_ref, xbuf_ref, qs_ref, ks_ref, vs_ref, gcb_ref, bb_ref, os_ref, st_ref, *, tc):
    it = pl.program_id(1)
    nchunk = tc // CHUNK
    hd = DN_HEAD_DIM

    @pl.when(it == 0)
    def _():
        xbuf_ref[:, 0:SUBLANES, :] = jnp.zeros((3, SUBLANES, D_DN), F32)
        st_ref[...] = jnp.zeros_like(st_ref)

    for j, (src, dst) in enumerate(((q_ref, qs_ref), (k_ref, ks_ref), (v_ref, vs_ref))):
        xbuf_ref[j, SUBLANES:, :] = src[...]
        acc = None
        for i in range(CONV_K):
            w = cw_ref[i:i + 1, j * D_DN:(j + 1) * D_DN]
            term = xbuf_ref[j, pl.ds(SUBLANES - (CONV_K - 1) + i, tc), :] * w
            acc = term if acc is None else acc + term
        dst[...] = _silu(acc)
        xbuf_ref[j, 0:SUBLANES, :] = xbuf_ref[j, tc:tc + SUBLANES, :]

    scale = DN_HEAD_DIM ** -0.5
    for h in range(DN_HEADS):
        ls = slice(h * hd, (h + 1) * hd)
        qh = qs_ref[:, ls]
        qs_ref[:, ls] = qh * (lax.rsqrt(jnp.sum(qh * qh, axis=-1, keepdims=True) + EPS) * scale)
        kh = ks_ref[:, ls]
        ks_ref[:, ls] = kh * lax.rsqrt(jnp.sum(kh * kh, axis=-1, keepdims=True) + EPS)

    ps = ps_ref[...]
    beta = _sigmoid(ps)
    xg = ps + gpar_ref[1:2, :]
    sp = jnp.maximum(xg, 0.0) + jnp.log1p(jnp.exp(-jnp.abs(xg)))
    g = -jnp.exp(gpar_ref[0:1, :]) * sp
    r = lax.broadcasted_iota(jnp.int32, (tc, tc), 0)
    c = lax.broadcasted_iota(jnp.int32, (tc, tc), 1)
    ltri = jnp.where(((r >> 6) == (c >> 6)) & (c <= r), 1.0, 0.0).astype(F32)
    gc = _dot_f32(ltri, g)
    gcb_ref[...] = _dot_f32(gc, expg_ref[...])
    bb_ref[...] = _dot_f32(beta, expb_ref[...])

    rr = lax.broadcasted_iota(jnp.int32, (CHUNK, CHUNK), 0)
    cc = lax.broadcasted_iota(jnp.int32, (CHUNK, CHUNK), 1)
    causal = cc <= rr
    strict = cc < rr
    eye = jnp.where(cc == rr, 1.0, 0.0).astype(F32)
    pick0 = jnp.where(cc == 0, 1.0, 0.0).astype(F32)
    neg_big = jnp.float32(-1e30)

    def chunk_body(n, carry):
        r0 = pl.multiple_of(n * CHUNK, CHUNK)
        rows = pl.ds(r0, CHUNK)
        for h in range(DN_HEADS):
            ls = slice(h * hd, (h + 1) * hd)
            qh = qs_ref[rows, ls]
            kh = ks_ref[rows, ls]
            vh = vs_ref[rows, ls]
            gcb = gcb_ref[rows, ls]
            betab = bb_ref[rows, ls]
            eg = jnp.exp(gcb)
            gcol = gcb[:, :CHUNK]
            grow = lax.dot_general(pick0, gcol, (((1,), (1,)), ((), ())),
                                   preferred_element_type=F32, precision=HIGHEST)
            decay = jnp.exp(jnp.where(causal, gcol - grow, neg_big))
            kk = _dot_nt(kh, kh)
            a_neg = jnp.where(strict, -(betab[:, :CHUNK] * kk * decay), 0.0)
            t_inv = eye + a_neg
            m = a_neg
            for _ in range(5):
                m = _dot(m, m)
                t_inv = t_inv + _dot(t_inv, m)
            rhs = jnp.concatenate([vh * betab, kh * (betab * eg)], axis=-1)
            sol = _dot(t_inv, rhs)
            u_c = sol[:, :hd]
            w_c = sol[:, hd:]
            qk = _dot_nt(qh, kh) * decay
            q_dec = qh * eg
            gl = gcb[CHUNK - 1:CHUNK, :]
            k_dec = kh * jnp.exp(gl - gcb)
            g_last = jnp.exp(gl)
            s = st_ref[h]
            ws_qs = _dot(jnp.concatenate([w_c, q_dec], axis=0), s)
            v_new = u_c - ws_qs[:CHUNK]
            os_ref[rows, ls] = ws_qs[CHUNK:] + _dot(qk, v_new)
            st_ref[h] = s * g_last + _dot_tn(k_dec, v_new)
        return carry

    lax.fori_loop(0, nchunk, chunk_body, 0)

    for h in range(DN_HEADS):
        ls = slice(h * hd, (h + 1) * hd)
        oh = os_ref[:, ls]
        ms = jnp.mean(oh * oh, axis=-1, keepdims=True)
        o_ref[:, ls] = (oh * lax.rsqrt(ms + EPS) * nw_ref[...] * _silu(z_ref[:, ls])).astype(o_ref.dtype)


def _dn_branch(proj, proj_small, batch, seq, conv_w, gpar, expand_b, expand_g, norm_w, tc=256):
    nt = seq // tc
    kern = functools.partial(_dn_kernel, tc=tc)
    row = lambda col: (lambda b, i: (b * nt + i, col))
    const = lambda b, i: (0, 0)
    return pl.pallas_call(
        kern,
        grid=(batch, nt),
        in_specs=[
            pl.BlockSpec((tc, D_DN), row(2)),
            pl.BlockSpec((tc, D_DN), row(3)),
            pl.BlockSpec((tc, D_DN), row(4)),
            pl.BlockSpec((tc, D_DN), row(5)),
            pl.BlockSpec((tc, SMALL_PAD), row(0)),
            pl.BlockSpec(conv_w.shape, const),
            pl.BlockSpec(gpar.shape, const),
            pl.BlockSpec(expand_b.shape, const),
            pl.BlockSpec(expand_g.shape, const),
            pl.BlockSpec(norm_w.shape, const),
        ],
        out_specs=pl.BlockSpec((tc, D_DN), row(0)),
        out_shape=jax.ShapeDtypeStruct((batch * seq, D_DN), BF16),
        scratch_shapes=[
            pltpu.VMEM((3, tc + SUBLANES, D_DN), F32),
            pltpu.VMEM((tc, D_DN), F32),
            pltpu.VMEM((tc, D_DN), F32),
            pltpu.VMEM((tc, D_DN), F32),
            pltpu.VMEM((tc, D_DN), F32),
            pltpu.VMEM((tc, D_DN), F32),
            pltpu.VMEM((tc, D_DN), F32),
            pltpu.VMEM((DN_HEADS, DN_HEAD_DIM, DN_HEAD_DIM), F32),
        ],
        compiler_params=pltpu.CompilerParams(
            dimension_semantics=("arbitrary", "arbitrary"), vmem_limit_bytes=VMEM_LIMIT),
        name="deltanet",
    )(proj, proj, proj, proj, proj_small, conv_w, gpar, expand_b, expand_g, norm_w)


def _merge_kernel(ys_ref, yd_ref, gs_ref, gd_ref, x_ref, wus_ref, wud_ref, wo_ref, fw_ref, o_ref):
    y_s = jnp.dot(ys_ref[...], wus_ref[...], preferred_element_type=F32)
    y_d = jnp.dot(yd_ref[...], wud_ref[...], preferred_element_type=F32)
    mixed = _sigmoid(gs_ref[...]) * y_s + _sigmoid(gd_ref[...]) * y_d
    xo = x_ref[...] + _dot(mixed, wo_ref[...])
    ms = jnp.mean(xo * xo, axis=-1, keepdims=True)
    o_ref[...] = xo * lax.rsqrt(ms + EPS) * fw_ref[...]


def _merge_out(ys, yd, proj, x2, w_us, w_ud, w_o, final_w, tm=256):
    t = x2.shape[0]
    gs_col = (2 * D_S5 + 4 * D_DN) // D_MODEL
    const = lambda i: (0, 0)
    return pl.pallas_call(
        _merge_kernel,
        grid=(t // tm,),
        in_specs=[
            pl.BlockSpec((tm, D_S5), lambda i: (i, 0)),
            pl.BlockSpec((tm, D_DN), lambda i: (i, 0)),
            pl.BlockSpec((tm, D_MODEL), lambda i: (i, gs_col)),
            pl.BlockSpec((tm, D_MODEL), lambda i: (i, gs_col + 1)),
            pl.BlockSpec((tm, D_MODEL), lambda i: (i, 0)),
            pl.BlockSpec(w_us.shape, const, pipeline_mode=pl.Buffered(1)),
            pl.BlockSpec(w_ud.shape, const, pipeline_mode=pl.Buffered(1)),
            pl.BlockSpec(w_o.shape, const, pipeline_mode=pl.Buffered(1)),
            pl.BlockSpec((1, D_MODEL), const),
        ],
        out_specs=pl.BlockSpec((tm, D_MODEL), lambda i: (i, 0)),
        out_shape=jax.ShapeDtypeStruct((t, D_MODEL), F32),
        compiler_params=pltpu.CompilerParams(
            dimension_semantics=("arbitrary",), vmem_limit_bytes=VMEM_LIMIT),
        name="merge_out",
    )(ys, yd, proj, proj, x2, w_us, w_ud, w_o, final_w)


def kernel(x, ln_w, w_in, s5_lam_re, s5_lam_im, s5_log_step, s5_b_re, s5_b_im, s5_c_re, s5_c_im,
           s5_d, s5_w_glu, s5_w_up, dn_conv_w, dn_a_log, dn_dt_bias, dn_norm_w, dn_w_up, w_out,
           final_norm_w):
    batch, seq, d = x.shape
    assert d == D_MODEL and ln_w.shape[0] == 1
    x2 = x.reshape(batch * seq, d)
    cut = 2 * D_S5 + 4 * D_DN

    w = w_in[0]
    w_main = jnp.concatenate([w[:, :cut], w[:, cut + N_SMALL:]], axis=1).astype(BF16)
    w_small = jnp.pad(w[:, cut:cut + N_SMALL], ((0, 0), (0, SMALL_PAD - N_SMALL))).astype(BF16)

    proj, proj_small = _in_proj(x2, ln_w, w_main, w_small)

    bb_re, bb_im, cc_re, cc_im, coef = _s5_weights(
        s5_lam_re[0], s5_lam_im[0], s5_log_step[0], s5_b_re[0], s5_b_im[0], s5_c_re[0], s5_c_im[0])
    ys = _s5_branch(proj, batch, seq, bb_re, bb_im, cc_re, cc_im, coef, s5_d, s5_w_glu[0].astype(BF16))

    lane = jnp.arange(SMALL_PAD)
    head_of_lane = jnp.arange(D_DN) // DN_HEAD_DIM
    expand_b = (lane[:, None] == head_of_lane[None, :]).astype(F32)
    expand_g = (lane[:, None] == head_of_lane[None, :] + DN_HEADS).astype(F32)
    zeros8 = jnp.zeros((DN_HEADS,), F32)
    pad_tail = jnp.zeros((SMALL_PAD - N_SMALL,), F32)
    gpar = jnp.stack([
        jnp.concatenate([zeros8, dn_a_log[0], pad_tail]),
        jnp.concatenate([zeros8, dn_dt_bias[0], pad_tail]),
    ])
    yd = _dn_branch(proj, proj_small, batch, seq, dn_conv_w[0], gpar, expand_b, expand_g, dn_norm_w)

    out = _merge_out(ys, yd, proj, x2, s5_w_up[0].astype(BF16), dn_w_up[0].astype(BF16),
                     w_out[0].astype(BF16), final_norm_w[None, :])
    return out.reshape(batch, seq, d)
```
